```python
import math
import jax, jax.numpy as jnp
from jax import lax
import numpy as np

D_MODEL = 1024
BATCH = 16
SEQ = 4096
DEPTH = 1

HEAD_DIM = 64
SB_HEADS = 8
SB_WIDTH = SB_HEADS * HEAD_DIM
RG_BLOCKS = 4
RG_BLOCK_DIM = 64
RG_WIDTH = RG_BLOCKS * RG_BLOCK_DIM
CA_HEADS = 4
CA_WIDTH = CA_HEADS * HEAD_DIM
D_MIX = SB_WIDTH + RG_WIDTH + CA_WIDTH
D_IN = 3 * SB_WIDTH + 2 * RG_WIDTH + CA_WIDTH
IN_SPLITS = (SB_WIDTH, 2 * SB_WIDTH, 3 * SB_WIDTH, 3 * SB_WIDTH + RG_WIDTH, 3 * SB_WIDTH + 2 * RG_WIDTH)
MEM_LEN = 256
CONV_WIDTH = 4
LRU_C = 8.0
SB_BLOCK = 128
N_EXPERTS = 64
TOP_K = 8
N_GROUPS = 8
TOPK_GROUPS = 4
D_EXPERT = 256
D_SHARED = 256
ROUTED_SCALE = 2.5
RMS_EPS = 1e-6

kernel_name = 'hybrid_stickbreak_rglru_memxattn_moe'


def rms_norm(x, g):
    xf = x.astype(jnp.float32)
    y = xf * lax.rsqrt(jnp.mean(xf * xf, axis=-1, keepdims=True) + RMS_EPS)
    return (y * g.astype(jnp.float32)).astype(x.dtype)


def split_heads(t, n_heads):
    b, s, _ = t.shape
    return t.reshape(b, s, n_heads, HEAD_DIM).transpose(0, 2, 1, 3)


def merge_heads(t):
    b, n, s, d = t.shape
    return t.transpose(0, 2, 1, 3).reshape(b, s, n * d)


def stick_breaking_attention(q, k, v):
    b, h, s, d = q.shape
    n_blocks = s // SB_BLOCK
    scale = 1.0 / math.sqrt(d)
    kf = k.astype(jnp.float32)
    vf = v.astype(jnp.float32)
    q_blocks = q.reshape(b, h, n_blocks, SB_BLOCK, d).transpose(2, 0, 1, 3, 4)
    key_pos = jnp.arange(s)

    def one_block(args):
        q_blk, blk = args
        z = jnp.einsum('bhqd,bhkd->bhqk', q_blk.astype(jnp.float32), kf) * scale
        query_pos = blk * SB_BLOCK + jnp.arange(SB_BLOCK)
        strict = key_pos[None, :] < query_pos[:, None]
        log_beta = jax.nn.log_sigmoid(z)
        log_not = jnp.where(strict, jax.nn.log_sigmoid(-z), 0.0)
        between = lax.cumsum(log_not, axis=3, reverse=True) - log_not
        w = jnp.where(strict, jnp.exp(log_beta + between), 0.0)
        return jnp.einsum('bhqk,bhkd->bhqd', w, vf)

    out = lax.map(one_block, (q_blocks, jnp.arange(n_blocks)))
    return out.transpose(1, 2, 0, 3, 4).reshape(b, h, s, d).astype(q.dtype)


def causal_depthwise_conv(x, w, bias):
    s = x.shape[1]
    xp = jnp.pad(x, ((0, 0), (CONV_WIDTH - 1, 0), (0, 0)))
    out = bias
    for tap in range(CONV_WIDTH):
        out = out + w[tap] * xp[:, tap:tap + s]
    return out


def rg_lru(x, w_rgate, b_rgate, w_igate, b_igate, lru_lambda):
    bsz, s, c = x.shape
    xf = x.astype(jnp.float32)
    xb = xf.reshape(bsz, s, RG_BLOCKS, RG_BLOCK_DIM)
    r = jax.nn.sigmoid(jnp.einsum('bsnc,ncd->bsnd', xb, w_rgate.astype(jnp.float32)).reshape(bsz, s, c)
                       + b_rgate.astype(jnp.float32))
    i = jax.nn.sigmoid(jnp.einsum('bsnc,ncd->bsnd', xb, w_igate.astype(jnp.float32)).reshape(bsz, s, c)
                       + b_igate.astype(jnp.float32))
    log_a = -LRU_C * r * jax.nn.softplus(-lru_lambda.astype(jnp.float32))
    a = jnp.exp(log_a)
    u = jnp.sqrt(-jnp.expm1(2.0 * log_a)) * (i * xf)

    def combine(left, right):
        a_l, h_l = left
        a_r, h_r = right
        return a_l * a_r, a_r * h_l + h_r

    _, h = lax.associative_scan(combine, (a, u), axis=1)
    return h


def memory_cross_attention(q, mem, norm_mem_g, w_mem_kv, q_norm_g, k_norm_g):
    mem_n = rms_norm(mem, norm_mem_g)
    k, v = jnp.split(mem_n @ w_mem_kv, 2, axis=-1)
    qh = rms_norm(split_heads(q, CA_HEADS), q_norm_g)
    kh = rms_norm(split_heads(k, CA_HEADS), k_norm_g)
    vh = split_heads(v, CA_HEADS)
    z = jnp.einsum('bhsd,bhmd->bhsm', qh.astype(jnp.float32), kh.astype(jnp.float32)) / math.sqrt(HEAD_DIM)
    p = jax.nn.softmax(z, axis=-1)
    o = jnp.einsum('bhsm,bhmd->bhsd', p, vh.astype(jnp.float32))
    return merge_heads(o).astype(q.dtype)


def hybrid_mixer(x, mem, norm_mix_g, norm_mem_g, w_in, conv_w, conv_b, w_rgate, b_rgate, w_igate,
                 b_igate, lru_lambda, w_mem_kv, q_norm_g, k_norm_g, g_out_sb, g_out_rg, g_out_ca, w_out):
    h = rms_norm(x, norm_mix_g)
    proj = h @ w_in
    q_sb, k_sb, v_sb, x_rg, gate_rg, q_ca = jnp.split(proj, IN_SPLITS, axis=-1)
    o_sb = merge_heads(stick_breaking_attention(split_heads(q_sb, SB_HEADS), split_heads(k_sb, SB_HEADS),
                                                split_heads(v_sb, SB_HEADS)))
    xc = causal_depthwise_conv(x_rg, conv_w, conv_b)
    hr = rg_lru(xc, w_rgate, b_rgate, w_igate, b_igate, lru_lambda)
    o_rg = (jax.nn.gelu(gate_rg.astype(jnp.float32)) * hr).astype(x.dtype)
    o_ca = memory_cross_attention(q_ca, mem, norm_mem_g, w_mem_kv, q_norm_g, k_norm_g)
    y = jnp.concatenate([rms_norm(o_sb, g_out_sb), rms_norm(o_rg, g_out_rg), rms_norm(o_ca, g_out_ca)], axis=-1)
    return y @ w_out


def moe_ffn(xn, w_router, router_bias, w_gate_e, w_up_e, w_down_e, w_gate_sh, w_up_sh, w_down_sh):
    bsz, s, d = xn.shape
    t = xn.reshape(bsz * s, d)
    n_tok = t.shape[0]
    scores = jax.nn.sigmoid(t.astype(jnp.float32) @ w_router.astype(jnp.float32))
    choice = scores + router_bias.astype(jnp.float32)
    grp = choice.reshape(n_tok, N_GROUPS, N_EXPERTS // N_GROUPS)
    grp_score = lax.top_k(grp, 2)[0].sum(axis=-1)
    _, gidx = lax.top_k(grp_score, TOPK_GROUPS)
    gmask = jax.nn.one_hot(gidx, N_GROUPS, dtype=jnp.float32).sum(axis=1) > 0
    emask = jnp.repeat(gmask, N_EXPERTS // N_GROUPS, axis=1)
    _, eidx = lax.top_k(jnp.where(emask, choice, -jnp.inf), TOP_K)
    w = jnp.take_along_axis(scores, eidx, axis=1)
    w = w / (w.sum(axis=-1, keepdims=True) + 1e-20) * ROUTED_SCALE
    rows = jnp.arange(n_tok)[:, None]
    gates = jnp.zeros((n_tok, N_EXPERTS), jnp.float32).at[rows, eidx].set(w)

    def add_expert(acc, params):
        wg, wu, wd, g = params
        hid = jax.nn.silu(t @ wg) * (t @ wu)
        return acc + g[:, None] * (hid @ wd).astype(jnp.float32), None

    routed, _ = lax.scan(add_expert, jnp.zeros((n_tok, d), jnp.float32),
                         (w_gate_e, w_up_e, w_down_e, gates.T))
    shared = (jax.nn.silu(t @ w_gate_sh) * (t @ w_up_sh)) @ w_down_sh
    return (routed + shared.astype(jnp.float32)).reshape(bsz, s, d).astype(xn.dtype)


def setup_inputs(seed: int = 0) -> dict:
    key = jax.random.key(seed)
    ks = jax.random.split(key, 28)
    f = jnp.float32
    L = DEPTH

    def nrm(k, shape, fan_in):
        return jax.random.normal(k, shape, f) * fan_in ** -0.5

    def gain(k, shape):
        return 1.0 + 0.02 * jax.random.normal(k, shape, f)

    def small(k, shape):
        return 0.02 * jax.random.normal(k, shape, f)

    u = jax.random.uniform(ks[11], (L, RG_WIDTH), f, 0.9, 0.999)
    a0 = u ** (1.0 / LRU_C)
    lru_lambda = jnp.log(a0) - jnp.log1p(-a0)
    return {
        'x': jax.random.normal(ks[0], (BATCH, SEQ, D_MODEL), f),
        'mem': jax.random.normal(ks[1], (BATCH, MEM_LEN, D_MODEL), f),
        'norm_mix_g': gain(ks[2], (L, D_MODEL)),
        'norm_mem_g': gain(ks[3], (L, D_MODEL)),
        'w_in': nrm(ks[4], (L, D_MODEL, D_IN), D_MODEL),
        'conv_w': nrm(ks[5], (L, CONV_WIDTH, RG_WIDTH), CONV_WIDTH),
        'conv_b': small(ks[6], (L, RG_WIDTH)),
        'w_rgate': nrm(ks[7], (L, RG_BLOCKS, RG_BLOCK_DIM, RG_BLOCK_DIM), RG_BLOCK_DIM),
        'b_rgate': small(ks[8], (L, RG_WIDTH)),
        'w_igate': nrm(ks[9], (L, RG_BLOCKS, RG_BLOCK_DIM, RG_BLOCK_DIM), RG_BLOCK_DIM),
        'b_igate': small(ks[10], (L, RG_WIDTH)),
        'lru_lambda': lru_lambda,
        'w_mem_kv': nrm(ks[12], (L, D_MODEL, 2 * CA_WIDTH), D_MODEL),
        'q_norm_g': gain(ks[13], (L, HEAD_DIM)),
        'k_norm_g': gain(ks[14], (L, HEAD_DIM)),
        'g_out_sb': gain(ks[15], (L, SB_WIDTH)),
        'g_out_rg': gain(ks[16], (L, RG_WIDTH)),
        'g_out_ca': gain(ks[17], (L, CA_WIDTH)),
        'w_out': nrm(ks[18], (L, D_MIX, D_MODEL), D_MIX),
        'norm_ffn_g': gain(ks[19], (L, D_MODEL)),
        'w_router': nrm(ks[20], (L, D_MODEL, N_EXPERTS), D_MODEL),
        'router_bias': 0.01 * jax.random.normal(ks[21], (L, N_EXPERTS), f),
        'w_gate_e': nrm(ks[22], (L, N_EXPERTS, D_MODEL, D_EXPERT), D_MODEL),
        'w_up_e': nrm(ks[23], (L, N_EXPERTS, D_MODEL, D_EXPERT), D_MODEL),
        'w_down_e': nrm(ks[24], (L, N_EXPERTS, D_EXPERT, D_MODEL), D_EXPERT),
        'w_gate_sh': nrm(ks[25], (L, D_MODEL, D_SHARED), D_MODEL),
        'w_up_sh': nrm(ks[26], (L, D_MODEL, D_SHARED), D_MODEL),
        'w_down_sh': nrm(ks[27], (L, D_SHARED, D_MODEL), D_SHARED),
    }


def reference(x, mem, norm_mix_g, norm_mem_g, w_in, conv_w, conv_b, w_rgate, b_rgate, w_igate, b_igate,
              lru_lambda, w_mem_kv, q_norm_g, k_norm_g, g_out_sb, g_out_rg, g_out_ca, w_out, norm_ffn_g,
              w_router, router_bias, w_gate_e, w_up_e, w_down_e, w_gate_sh, w_up_sh, w_down_sh):
    for layer in range(DEPTH):
        x = x + hybrid_mixer(x, mem, norm_mix_g[layer], norm_mem_g[layer], w_in[layer], conv_w[layer],
                             conv_b[layer], w_rgate[layer], b_rgate[layer], w_igate[layer], b_igate[layer],
                             lru_lambda[layer], w_mem_kv[layer], q_norm_g[layer], k_norm_g[layer],
                             g_out_sb[layer], g_out_rg[layer], g_out_ca[layer], w_out[layer])
        x = x + moe_ffn(rms_norm(x, norm_ffn_g[layer]), w_router[layer], router_bias[layer], w_gate_e[layer],
                        w_up_e[layer], w_down_e[layer], w_gate_sh[layer], w_up_sh[layer], w_down_sh[layer])
    return x
```

```python
import functools
import math

import jax
import jax.numpy as jnp
from jax import lax
from jax.experimental import pallas as pl
from jax.experimental.pallas import tpu as pltpu

F32 = jnp.float32
BF16 = jnp.bfloat16

HEAD_DIM = 64
SB_HEADS = 8
SB_WIDTH = SB_HEADS * HEAD_DIM
RG_BLOCKS = 4
RG_WIDTH = 256
CA_HEADS = 4
CA_WIDTH = CA_HEADS * HEAD_DIM
CONV_WIDTH = 4
LRU_C = 8.0
N_EXPERTS = 64
TOP_K = 8
N_GROUPS = 8
GROUP_SIZE = N_EXPERTS // N_GROUPS
TOPK_GROUPS = 4
D_EXPERT = 256
ROUTED_SCALE = 2.5
RMS_EPS = 1e-6

LANES = 128
VMEM_LIMIT = 56 * 1024 * 1024
SB_UNDERFLOW = -104.0
NEG_INF = float("-inf")


def _split_bf16(a):
    hi = a.astype(BF16)
    lo = (a - hi.astype(F32)).astype(BF16)
    return hi, lo


def _dot(a, b):
    return jnp.dot(a, b, preferred_element_type=F32)


def _dot_nt(a, b):
    return lax.dot_general(a, b, (((1,), (1,)), ((), ())), preferred_element_type=F32)


def _rms(x, g):
    ms = jnp.mean(x * x, axis=-1, keepdims=True)
    return x * lax.rsqrt(ms + RMS_EPS) * g


def _sigmoid(x):
    return 1.0 / (1.0 + jnp.exp(-x))


def _head_mean_matrix(width):
    r = lax.broadcasted_iota(jnp.int32, (width, width), 0) // HEAD_DIM
    c = lax.broadcasted_iota(jnp.int32, (width, width), 1) // HEAD_DIM
    return jnp.where(r == c, 1.0 / HEAD_DIM, 0.0).astype(BF16)


def _head_rms(x, g):
    hi, lo = _split_bf16(x * x)
    m = _head_mean_matrix(x.shape[-1])
    ms = _dot(hi, m) + _dot(lo, m)
    return x * lax.rsqrt(ms + RMS_EPS) * g


def _in_proj_kernel(x_ref, g_ref, w_ref, qkv_ref, rg_ref, qca_ref):
    h = _rms(x_ref[...], g_ref[...]).astype(BF16)
    n_qkv = qkv_ref.shape[-1]
    n_rg = rg_ref.shape[-1]
    qkv_ref[...] = _dot(h, w_ref[:, :n_qkv]).astype(BF16)
    rg_ref[...] = _dot(h, w_ref[:, n_qkv:n_qkv + n_rg])
    qca_ref[...] = _dot(h, w_ref[:, n_qkv + n_rg:]).astype(BF16)


def _in_proj(x2, g, w, tm):
    t, d = x2.shape
    n_qkv, n_rg, n_ca = 3 * SB_WIDTH, 2 * RG_WIDTH, CA_WIDTH
    return pl.pallas_call(
        _in_proj_kernel,
        grid=(t // tm,),
        in_specs=[
            pl.BlockSpec((tm, d), lambda i: (i, 0)),
            pl.BlockSpec((1, d), lambda i: (0, 0)),
            pl.BlockSpec((d, n_qkv + n_rg + n_ca), lambda i: (0, 0)),
        ],
        out_specs=[
            pl.BlockSpec((tm, n_qkv), lambda i: (i, 0)),
            pl.BlockSpec((tm, n_rg), lambda i: (i, 0)),
            pl.BlockSpec((tm, n_ca), lambda i: (i, 0)),
        ],
        out_shape=[
            jax.ShapeDtypeStruct((t, n_qkv), BF16),
            jax.ShapeDtypeStruct((t, n_rg), F32),
            jax.ShapeDtypeStruct((t, n_ca), BF16),
        ],
        compiler_params=pltpu.CompilerParams(
            dimension_semantics=("arbitrary",), vmem_limit_bytes=VMEM_LIMIT),
        name="in_proj",
    )(x2, g, w)


def _mem_kv_kernel(m_ref, g_ref, w_ref, kg_ref, k_ref, v_ref):
    h = _rms(m_ref[0], g_ref[...]).astype(BF16)
    kv = _dot(h, w_ref[...])
    k_ref[0] = _head_rms(kv[:, :CA_WIDTH], kg_ref[...]).astype(BF16)
    v_ref[0] = kv[:, CA_WIDTH:].astype(BF16)


def _mem_kv(mem, g, w, kg):
    b, m, d = mem.shape
    return pl.pallas_call(
        _mem_kv_kernel,
        grid=(b,),
        in_specs=[
            pl.BlockSpec((1, m, d), lambda i: (i, 0, 0)),
            pl.BlockSpec((1, d), lambda i: (0, 0)),
            pl.BlockSpec((d, 2 * CA_WIDTH), lambda i: (0, 0)),
            pl.BlockSpec((1, CA_WIDTH), lambda i: (0, 0)),
        ],
        out_specs=[
            pl.BlockSpec((1, m, CA_WIDTH), lambda i: (i, 0, 0)),
            pl.BlockSpec((1, m, CA_WIDTH), lambda i: (i, 0, 0)),
        ],
        out_shape=[
            jax.ShapeDtypeStruct((b, m, CA_WIDTH), BF16),
            jax.ShapeDtypeStruct((b, m, CA_WIDTH), BF16),
        ],
        compiler_params=pltpu.CompilerParams(
            dimension_semantics=("arbitrary",), vmem_limit_bytes=VMEM_LIMIT),
        name="mem_kv",
    )(mem, g, w, kg)


def _sb_block(q_pair, k_pair, v_pair, carry2, acc, cum_rhs, lo_lane, v_lo_lane, diag_mask):
    zero = jnp.zeros_like(q_pair)
    tk = k_pair.shape[0]
    ws = []
    new_carry = []
    for half in range(2):
        q_h = jnp.where(lo_lane, q_pair, zero) if half == 0 else jnp.where(lo_lane, zero, q_pair)
        s = _dot_nt(q_h, k_pair)
        sp = jnp.log(1.0 + jnp.exp(-jnp.abs(s)))
        log_beta = jnp.minimum(s, 0.0) - sp
        log_not = log_beta - s
        if diag_mask is not None:
            log_not = jnp.where(diag_mask, log_not, 0.0)
        hi, lo = _split_bf16(log_not)
        cs = _dot(jnp.concatenate([hi, lo], axis=1), cum_rhs)
        w = jnp.exp(log_beta + cs[:, :tk] + carry2[half])
        if diag_mask is not None:
            w = jnp.where(diag_mask, w, 0.0)
        ws.append(w.astype(BF16))
        new_carry.append(carry2[half] + cs[:, tk:])
    vz = jnp.zeros_like(v_pair)
    v_bd = jnp.concatenate(
        [jnp.where(v_lo_lane, v_pair, vz), jnp.where(v_lo_lane, vz, v_pair)], axis=0)
    acc = acc + _dot(jnp.concatenate(ws, axis=1), v_bd)
    return new_carry, acc


def _sb_kernel(q_ref, k_ref, v_ref, g_ref, o_ref, acc_ref, carry_ref, *, tq):
    i = pl.program_id(1)
    tk = tq
    n_pairs = SB_HEADS // 2
    r = lax.broadcasted_iota(jnp.int32, (2 * tk, tk + LANES), 0) % tk
    c = lax.broadcasted_iota(jnp.int32, (2 * tk, tk + LANES), 1)
    cum_rhs = jnp.where((r > c) | (c >= tk), 1.0, 0.0).astype(BF16)
    lo_lane = lax.broadcasted_iota(jnp.int32, (tq, LANES), 1) < HEAD_DIM
    v_lo_lane = lax.broadcasted_iota(jnp.int32, (tk, LANES), 1) < HEAD_DIM
    strict = (lax.broadcasted_iota(jnp.int32, (tq, tk), 1)
              < lax.broadcasted_iota(jnp.int32, (tq, tk), 0))
    scale = 1.0 / math.sqrt(HEAD_DIM)

    def q_pair(p):
        return (q_ref[:, p * LANES:(p + 1) * LANES].astype(F32) * scale).astype(BF16)

    def step(j, diag_mask):
        row0 = pl.multiple_of(j * tk, tk)
        cmax = None
        for p in range(n_pairs):
            k_pair = k_ref[pl.ds(row0, tk), p * LANES:(p + 1) * LANES]
            v_pair = v_ref[pl.ds(row0, tk), p * LANES:(p + 1) * LANES]
            carry2 = [carry_ref[2 * p], carry_ref[2 * p + 1]]
            acc = acc_ref[:, p * LANES:(p + 1) * LANES]
            carry2, acc = _sb_block(q_pair(p), k_pair, v_pair, carry2, acc, cum_rhs, lo_lane,
                                    v_lo_lane, diag_mask)
            carry_ref[2 * p] = carry2[0]
            carry_ref[2 * p + 1] = carry2[1]
            acc_ref[:, p * LANES:(p + 1) * LANES] = acc
            m = jnp.maximum(jnp.max(carry2[0]), jnp.max(carry2[1]))
            cmax = m if cmax is None else jnp.maximum(cmax, m)
        return cmax

    acc_ref[...] = jnp.zeros_like(acc_ref)
    carry_ref[...] = jnp.zeros_like(carry_ref)
    cmax0 = step(i, strict)

    def cond(state):
        j, cmax = state
        return jnp.logical_and(j >= 0, cmax > SB_UNDERFLOW)

    def body(state):
        j, _ = state
        return j - 1, step(j, None)

    lax.while_loop(cond, body, (i - 1, cmax0))
    o_ref[...] = _rms(acc_ref[...], g_ref[...]).astype(BF16)


def _sb_attn(qkv, g, batch, seq, tq):
    nq = seq // tq
    t = batch * seq
    return pl.pallas_call(
        functools.partial(_sb_kernel, tq=tq),
        grid=(batch, nq),
        in_specs=[
            pl.BlockSpec((tq, SB_WIDTH), lambda b, i: (b * nq + i, 0)),
            pl.BlockSpec((seq, SB_WIDTH), lambda b, i: (b, 1)),
            pl.BlockSpec((seq, SB_WIDTH), lambda b, i: (b, 2)),
            pl.BlockSpec((1, SB_WIDTH), lambda b, i: (0, 0)),
        ],
        out_specs=pl.BlockSpec((tq, SB_WIDTH), lambda b, i: (b * nq + i, 0)),
        out_shape=jax.ShapeDtypeStruct((t, SB_WIDTH), BF16),
        scratch_shapes=[
            pltpu.VMEM((tq, SB_WIDTH), F32),
            pltpu.VMEM((SB_HEADS, tq, LANES), F32),
        ],
        compiler_params=pltpu.CompilerParams(
            dimension_semantics=("arbitrary", "arbitrary"), vmem_limit_bytes=VMEM_LIMIT),
        name="sb_attn",
    )(qkv, qkv, qkv, g)


def _shift_rows(x, d, fill):
    n = x.shape[0]
    if d % 8 == 0:
        return jnp.concatenate([jnp.full((d,) + x.shape[1:], fill, x.dtype), x[:n - d]], axis=0)
    rows = lax.broadcasted_iota(jnp.int32, x.shape, 0)
    return jnp.where(rows < d, fill, pltpu.roll(x, d, axis=0))


def _rglru_kernel(rg_ref, cw_ref, cb_ref, wr_hi_ref, wr_lo_ref, br_ref, wi_hi_ref, wi_lo_ref,
                  bi_ref, lam_ref, g_ref, o_ref, tail_ref, h_ref, *, tc):
    c = pl.program_id(1)

    @pl.when(c == 0)
    def _():
        tail_ref[...] = jnp.zeros_like(tail_ref)
        h_ref[...] = jnp.zeros_like(h_ref)

    x = rg_ref[:, :RG_WIDTH]
    gate = rg_ref[:, RG_WIDTH:]
    tail = tail_ref[...]
    cw = cw_ref[...]
    xc = cb_ref[...] + cw[CONV_WIDTH - 1:CONV_WIDTH, :] * x
    rows8 = lax.broadcasted_iota(jnp.int32, tail.shape, 0)
    for d in range(1, CONV_WIDTH):
        rolled = pltpu.roll(x, d, axis=0)
        head = jnp.where(rows8 < d, pltpu.roll(tail, d, axis=0), rolled[:8])
        xd = jnp.concatenate([head, rolled[8:]], axis=0)
        xc = xc + cw[CONV_WIDTH - 1 - d:CONV_WIDTH - d, :] * xd
    tail_ref[...] = x[tc - 8:, :]

    x_hi, x_lo = _split_bf16(xc)

    def gate_dot(w_hi_ref, w_lo_ref):
        w_hi = w_hi_ref[...]
        return _dot(x_hi, w_hi) + _dot(x_lo, w_hi) + _dot(x_hi, w_lo_ref[...])

    r = _sigmoid(gate_dot(wr_hi_ref, wr_lo_ref) + br_ref[...])
    ig = _sigmoid(gate_dot(wi_hi_ref, wi_lo_ref) + bi_ref[...])
    lam = lam_ref[...]
    softplus_neg = jnp.maximum(-lam, 0.0) + jnp.log(1.0 + jnp.exp(-jnp.abs(lam)))
    log_a = -LRU_C * r * softplus_neg
    a = jnp.exp(log_a)
    th = jnp.tanh(log_a)
    u = jnp.sqrt(-2.0 * th / (1.0 - th)) * (ig * xc)

    aa, hh = a, u
    d = 1
    while d < tc:
        a_sh = _shift_rows(aa, d, 1.0)
        h_sh = _shift_rows(hh, d, 0.0)
        hh = hh + aa * h_sh
        aa = aa * a_sh
        d *= 2
    h = hh + aa * h_ref[...]
    h_ref[...] = h[tc - 1:tc, :]

    gelu = 0.5 * gate * (1.0 + jnp.tanh(math.sqrt(2.0 / math.pi) * (gate + 0.044715 * gate * gate * gate)))
    o_ref[...] = _rms(gelu * h, g_ref[...]).astype(BF16)


def _rglru(rg, cw, cb, wr_hi, wr_lo, br, wi_hi, wi_lo, bi, lam, g, batch, seq, tc):
    nc = seq // tc
    t = batch * seq
    small = lambda shape: pl.BlockSpec(shape, lambda b, c: (0, 0))
    return pl.pallas_call(
        functools.partial(_rglru_kernel, tc=tc),
        grid=(batch, nc),
        in_specs=[
            pl.BlockSpec((tc, 2 * RG_WIDTH), lambda b, c: (b * nc + c, 0)),
            small((CONV_WIDTH, RG_WIDTH)), small((1, RG_WIDTH)),
            small((RG_WIDTH, RG_WIDTH)), small((RG_WIDTH, RG_WIDTH)), small((1, RG_WIDTH)),
            small((RG_WIDTH, RG_WIDTH)), small((RG_WIDTH, RG_WIDTH)), small((1, RG_WIDTH)),
            small((1, RG_WIDTH)), small((1, RG_WIDTH)),
        ],
        out_specs=pl.BlockSpec((tc, RG_WIDTH), lambda b, c: (b * nc + c, 0)),
        out_shape=jax.ShapeDtypeStruct((t, RG_WIDTH), BF16),
        scratch_shapes=[pltpu.VMEM((8, RG_WIDTH), F32), pltpu.VMEM((1, RG_WIDTH), F32)],
        compiler_params=pltpu.CompilerParams(
            dimension_semantics=("arbitrary", "arbitrary"), vmem_limit_bytes=VMEM_LIMIT),
        name="rglru",
    )(rg, cw, cb, wr_hi, wr_lo, br, wi_hi, wi_lo, bi, lam, g)


def _cross_attn_kernel(q_ref, k_ref, v_ref, qg_ref, g_ref, o_ref):
    q = _head_rms(q_ref[...].astype(F32), qg_ref[...])
    q = (q * (1.0 / math.sqrt(HEAD_DIM))).astype(BF16)
    k = k_ref[0]
    v = v_ref[0]
    lane_head = lax.broadcasted_iota(jnp.int32, q.shape, 1) // HEAD_DIM
    v_head = lax.broadcasted_iota(jnp.int32, v.shape, 1) // HEAD_DIM
    acc = jnp.zeros(q.shape, F32)
    for h in range(CA_HEADS):
        z = _dot_nt(jnp.where(lane_head == h, q, jnp.zeros_like(q)), k)
        z = z - jnp.max(z, axis=-1, keepdims=True)
        p = jnp.exp(z)
        p = p / jnp.sum(p, axis=-1, keepdims=True)
        acc = acc + _dot(p.astype(BF16), jnp.where(v_head == h, v, jnp.zeros_like(v)))
    o_ref[...] = _rms(acc, g_ref[...]).astype(BF16)


def _cross_attn(qca, kh, vh, qg, g, batch, seq, tq):
    nq = seq // tq
    t = batch * seq
    m = kh.shape[1]
    return pl.pallas_call(
        _cross_attn_kernel,
        grid=(batch, nq),
        in_specs=[
            pl.BlockSpec((tq, CA_WIDTH), lambda b, i: (b * nq + i, 0)),
            pl.BlockSpec((1, m, CA_WIDTH), lambda b, i: (b, 0, 0)),
            pl.BlockSpec((1, m, CA_WIDTH), lambda b, i: (b, 0, 0)),
            pl.BlockSpec((1, CA_WIDTH), lambda b, i: (0, 0)),
            pl.BlockSpec((1, CA_WIDTH), lambda b, i: (0, 0)),
        ],
        out_specs=pl.BlockSpec((tq, CA_WIDTH), lambda b, i: (b * nq + i, 0)),
        out_shape=jax.ShapeDtypeStruct((t, CA_WIDTH), BF16),
        compiler_params=pltpu.CompilerParams(
            dimension_semantics=("arbitrary", "arbitrary"), vmem_limit_bytes=VMEM_LIMIT),
        name="cross_attn",
    )(qca, kh, vh, qg, g)


def _route(logits_t, bias_t):
    n = logits_t.shape[-1]
    shape = (N_GROUPS, GROUP_SIZE, n)
    scores = _sigmoid(logits_t).reshape(shape)
    choice = scores + bias_t.reshape(N_GROUPS, GROUP_SIZE, 1)
    widx = lax.broadcasted_iota(jnp.int32, shape, 1).astype(F32)
    gidx = lax.broadcasted_iota(jnp.int32, shape, 0).astype(F32)
    eidx = gidx * GROUP_SIZE + widx

    def gmax(v):
        return jnp.max(jnp.max(v, axis=1, keepdims=True), axis=0, keepdims=True)

    def gmin(v):
        return jnp.min(jnp.min(v, axis=1, keepdims=True), axis=0, keepdims=True)

    m1 = jnp.max(choice, axis=1, keepdims=True)
    i1 = jnp.min(jnp.where(choice == m1, widx, float(GROUP_SIZE)), axis=1, keepdims=True)
    m2 = jnp.max(jnp.where(widx == i1, NEG_INF, choice), axis=1, keepdims=True)
    gscore = jnp.broadcast_to(m1 + m2, shape)

    gsel = jnp.zeros(shape, jnp.bool_)
    cur = gscore
    for _ in range(TOPK_GROUPS):
        m = jnp.max(cur, axis=0, keepdims=True)
        ig = jnp.min(jnp.where(cur == m, gidx, float(N_GROUPS)), axis=0, keepdims=True)
        pick = gidx == ig
        gsel = jnp.logical_or(gsel, pick)
        cur = jnp.where(pick, NEG_INF, cur)

    cur = jnp.where(gsel, choice, NEG_INF)
    esel = jnp.zeros(shape, jnp.bool_)
    for _ in range(TOP_K):
        m = gmax(cur)
        ie = gmin(jnp.where(cur == m, eidx, float(N_EXPERTS)))
        pick = eidx == ie
        esel = jnp.logical_or(esel, pick)
        cur = jnp.where(pick, NEG_INF, cur)

    w = jnp.where(esel, scores, 0.0)
    denom = jnp.sum(jnp.sum(w, axis=1, keepdims=True), axis=0, keepdims=True) + 1e-20
    gates = w / denom * ROUTED_SCALE
    return gates.reshape(N_EXPERTS, n)


def _out_route_kernel(x_ref, ysb_ref, yrg_ref, yca_ref, wo_ref, gf_ref, wr_hi_ref, wr_lo_ref,
                      rb_ref, x1_ref, xn_ref, gates_ref):
    mix = (_dot(ysb_ref[...], wo_ref[:SB_WIDTH, :])
           + _dot(yrg_ref[...], wo_ref[SB_WIDTH:SB_WIDTH + RG_WIDTH, :])
           + _dot(yca_ref[...], wo_ref[SB_WIDTH + RG_WIDTH:, :]))
    x1 = x_ref[...] + mix
    x1_ref[...] = x1
    xn = _rms(x1, gf_ref[...])
    xn_hi, xn_lo = _split_bf16(xn)
    xn_ref[...] = xn_hi
    wr_hi = wr_hi_ref[...]
    logits_t = _dot_nt(wr_hi, xn_hi) + _dot_nt(wr_hi, xn_lo) + _dot_nt(wr_lo_ref[...], xn_hi)
    gates_t = _route(logits_t, rb_ref[...])
    tm = gates_t.shape[-1]
    row = lax.broadcasted_iota(jnp.int32, (LANES - N_EXPERTS, tm), 0)
    pad = jnp.where(row == 0, 1.0, 0.0).astype(F32)
    gates_ref[...] = jnp.concatenate([gates_t, pad], axis=0).T


def _out_route(x2, ysb, yrg, yca, wo, gf, wr_hi, wr_lo, rb, tm):
    t, d = x2.shape
    row = lambda n: pl.BlockSpec((tm, n), lambda i: (i, 0))
    full = lambda a: pl.BlockSpec(a.shape, lambda i: (0, 0))
    return pl.pallas_call(
        _out_route_kernel,
        grid=(t // tm,),
        in_specs=[row(d), row(SB_WIDTH), row(RG_WIDTH), row(CA_WIDTH), full(wo), full(gf),
                  full(wr_hi), full(wr_lo), full(rb)],
        out_specs=[row(d), row(d), row(LANES)],
        out_shape=[
            jax.ShapeDtypeStruct((t, d), F32),
            jax.ShapeDtypeStruct((t, d), BF16),
            jax.ShapeDtypeStruct((t, LANES), F32),
        ],
        compiler_params=pltpu.CompilerParams(
            dimension_semantics=("arbitrary",), vmem_limit_bytes=VMEM_LIMIT),
        name="out_route",
    )(x2, ysb, yrg, yca, wo, gf, wr_hi, wr_lo, rb)


def _moe_kernel(xn_ref, gates_ref, x1_ref, wgu_ref, wd_ref, o_ref, acc_ref):
    e = pl.program_id(1)

    @pl.when(e == 0)
    def _():
        acc_ref[...] = x1_ref[...]

    h = _dot(xn_ref[...], wgu_ref[0])
    sel = (lax.broadcasted_iota(jnp.int32, (LANES, D_EXPERT), 0) == e)
    sel = jnp.where(sel, 1.0, 0.0).astype(BF16)
    g_hi, g_lo = _split_bf16(gates_ref[...])
    gate = _dot(g_hi, sel) + _dot(g_lo, sel)
    hg = h[:, :D_EXPERT]
    hid = hg * _sigmoid(hg) * h[:, D_EXPERT:] * gate
    acc_ref[...] += _dot(hid.astype(BF16), wd_ref[0])

    @pl.when(e == pl.num_programs(1) - 1)
    def _():
        o_ref[...] = acc_ref[...]


def _moe(xn, gates, x1, wgu, wd, tm):
    t, d = x1.shape
    n_e = wgu.shape[0]
    return pl.pallas_call(
        _moe_kernel,
        grid=(t // tm, n_e),
        in_specs=[
            pl.BlockSpec((tm, d), lambda i, e: (i, 0)),
            pl.BlockSpec((tm, LANES), lambda i, e: (i, 0)),
            pl.BlockSpec((tm, d), lambda i, e: (i, 0)),
            pl.BlockSpec((1, d, 2 * D_EXPERT), lambda i, e: (e, 0, 0)),
            pl.BlockSpec((1, D_EXPERT, d), lambda i, e: (e, 0, 0)),
        ],
        out_specs=pl.BlockSpec((tm, d), lambda i, e: (i, 0)),
        out_shape=jax.ShapeDtypeStruct((t, d), F32),
        scratch_shapes=[pltpu.VMEM((tm, d), F32)],
        compiler_params=pltpu.CompilerParams(
            dimension_semantics=("arbitrary", "arbitrary"), vmem_limit_bytes=VMEM_LIMIT),
        name="moe",
    )(xn, gates, x1, wgu, wd)


def _block_diag(w):
    n, c, _ = w.shape
    eye = jnp.eye(n, dtype=w.dtype)
    return jnp.einsum("ncd,nm->ncmd", w, eye).reshape(n * c, n * c)


def _tile(n, pref):
    t = min(n, pref)
    assert n % t == 0, (n, t)
    return t


def _layer(x2, mem, batch, seq, norm_mix_g, norm_mem_g, w_in, conv_w, conv_b, w_rgate, b_rgate,
           w_igate, b_igate, lru_lambda, w_mem_kv, q_norm_g, k_norm_g, g_out_sb, g_out_rg,
           g_out_ca, w_out, norm_ffn_g, w_router, router_bias, w_gate_e, w_up_e, w_down_e,
           w_gate_sh, w_up_sh, w_down_sh):
    t = batch * seq
    row = lambda v: v.reshape(1, -1).astype(F32)

    qkv, rg, qca = _in_proj(x2, row(norm_mix_g), w_in.astype(BF16), _tile(t, 512))
    kh, vh = _mem_kv(mem, row(norm_mem_g), w_mem_kv.astype(BF16),
                     row(jnp.tile(k_norm_g, CA_HEADS)))
    ysb = _sb_attn(qkv, row(g_out_sb), batch, seq, _tile(seq, 128))
    wr_hi, wr_lo = _split_bf16(_block_diag(w_rgate))
    wi_hi, wi_lo = _split_bf16(_block_diag(w_igate))
    yrg = _rglru(rg, conv_w.astype(F32), row(conv_b), wr_hi, wr_lo, row(b_rgate), wi_hi, wi_lo,
                 row(b_igate), row(lru_lambda), row(g_out_rg), batch, seq, _tile(seq, 512))
    yca = _cross_attn(qca, kh, vh, row(jnp.tile(q_norm_g, CA_HEADS)), row(g_out_ca), batch, seq,
                      _tile(seq, 512))
    rt_hi, rt_lo = _split_bf16(w_router.T)
    x1, xn, gates = _out_route(x2, ysb, yrg, yca, w_out.astype(BF16), row(norm_ffn_g), rt_hi,
                               rt_lo, router_bias.reshape(-1, 1).astype(F32), _tile(t, 512))
    wgu = jnp.concatenate([
        jnp.concatenate([w_gate_e, w_up_e], axis=-1),
        jnp.concatenate([w_gate_sh, w_up_sh], axis=-1)[None]], axis=0).astype(BF16)
    wd = jnp.concatenate([w_down_e, w_down_sh[None]], axis=0).astype(BF16)
    return _moe(xn, gates, x1, wgu, wd, _tile(t, 1024))


def kernel(x, mem, norm_mix_g, norm_mem_g, w_in, conv_w, conv_b, w_rgate, b_rgate, w_igate, b_igate, lru_lambda, w_mem_kv, q_norm_g, k_norm_g, g_out_sb, g_out_rg, g_out_ca, w_out, norm_ffn_g, w_router, router_bias, w_gate_e, w_up_e, w_down_e, w_gate_sh, w_up_sh, w_down_sh):
    batch, seq, d = x.shape
    x2 = x.reshape(batch * seq, d)
    for layer in range(norm_mix_g.shape[0]):
        x2 = _layer(x2, mem, batch, seq, norm_mix_g[layer], norm_mem_g[layer], w_in[layer],
                    conv_w[layer], conv_b[layer], w_rgate[layer], b_rgate[layer], w_igate[layer],
                    b_igate[layer], lru_lambda[layer], w_mem_kv[layer], q_norm_g[layer],
                    k_norm_g[layer], g_out_sb[layer], g_out_rg[layer], g_out_ca[layer],
                    w_out[layer], norm_ffn_g[layer], w_router[layer], router_bias[layer],
                    w_gate_e[layer], w_up_e[layer], w_down_e[layer], w_gate_sh[layer],
                    w_up_sh[layer], w_down_sh[layer])
    return x2.reshape(batch, seq, d)
```
